```python
import jax, jax.numpy as jnp
from jax import lax
import numpy as np

D_MODEL = 1024
BATCH = 8
SEQ = 8192
DEPTH = 4

D_MIX = 2 * D_MODEL
GLA_HEADS = 4
GLA_VWIDTH = D_MODEL
GLA_KWIDTH = D_MODEL // 2
GLA_DK = GLA_KWIDTH // GLA_HEADS
GLA_DV = GLA_VWIDTH // GLA_HEADS
GLA_GATE_RANK = 16
GLA_GATE_NORMALIZER = 16.0
GLA_CHUNK = 64
SC_WIDTH = D_MODEL // 2
SC_CONV = 3
CF_WIDTH = D_MODEL // 2
CF_CONV = 31

_SPLIT_SIZES = (
    GLA_KWIDTH,
    GLA_KWIDTH,
    GLA_VWIDTH,
    GLA_VWIDTH,
    GLA_GATE_RANK,
    SC_WIDTH,
    SC_WIDTH,
    SC_WIDTH,
    SC_WIDTH,
    CF_WIDTH,
    CF_WIDTH,
    CF_WIDTH,
)
D_IN = 2 * GLA_KWIDTH + 2 * GLA_VWIDTH + GLA_GATE_RANK + 4 * SC_WIDTH + 3 * CF_WIDTH

NORM_EPS = 1e-6
LN_EPS = 1e-5

kernel_name = "hybrid_gla_shortconv_conformer_trunk"


def split_points():
    pts = []
    acc = 0
    for s in _SPLIT_SIZES[:-1]:
        acc += s
        pts.append(acc)
    return tuple(pts)


def rms_norm(x, w, eps=NORM_EPS):
    xf = x.astype(jnp.float32)
    y = xf * lax.rsqrt(jnp.mean(xf * xf, axis=-1, keepdims=True) + eps)
    return (y * w.astype(jnp.float32)).astype(x.dtype)


def layer_norm(x, w, b, eps=LN_EPS):
    xf = x.astype(jnp.float32)
    mu = jnp.mean(xf, axis=-1, keepdims=True)
    xc = xf - mu
    var = jnp.mean(xc * xc, axis=-1, keepdims=True)
    y = xc * lax.rsqrt(var + eps) * w.astype(jnp.float32) + b.astype(jnp.float32)
    return y.astype(x.dtype)


def causal_depthwise_conv(u, w, b=None):
    k_width, ch = w.shape
    y = lax.conv_general_dilated(
        u, w[:, None, :].astype(u.dtype),
        window_strides=(1,),
        padding=[(k_width - 1, 0)],
        dimension_numbers=("NWC", "WIO", "NWC"),
        feature_group_count=ch)
    if b is not None:
        y = y + b.astype(y.dtype)
    return y


def gla_chunked(q, k, v, log_alpha):
    bsz, t_len, n_h, dk = q.shape
    dv = v.shape[-1]
    n_chunks = t_len // GLA_CHUNK

    def to_chunks(t):
        return t.reshape(bsz, n_chunks, GLA_CHUNK, n_h, t.shape[-1]).transpose(0, 3, 1, 2, 4).astype(jnp.float32)

    qc = to_chunks(q) * (dk ** -0.5)
    kc = to_chunks(k)
    vc = to_chunks(v)
    b = jnp.cumsum(to_chunks(log_alpha), axis=3)
    b_last = b[:, :, :, -1:, :]
    b_ref = b[:, :, :, GLA_CHUNK // 2:GLA_CHUNK // 2 + 1, :]

    q_intra = qc * jnp.exp(b - b_ref)
    k_intra = kc * jnp.exp(b_ref - b)
    scores = jnp.einsum("bhncd,bhnsd->bhncs", q_intra, k_intra)
    causal = jnp.tril(jnp.ones((GLA_CHUNK, GLA_CHUNK), dtype=bool))
    scores = jnp.where(causal, scores, 0.0)
    o_intra = jnp.einsum("bhncs,bhnse->bhnce", scores, vc)

    q_inter = qc * jnp.exp(b)
    k_inter = kc * jnp.exp(b_last - b)
    chunk_decay = jnp.exp(b_last[:, :, :, 0, :])

    def step(state, xs):
        qi, ki, vi, di = xs
        o = jnp.einsum("bhcd,bhde->bhce", qi, state)
        state = state * di[..., None] + jnp.einsum("bhcd,bhce->bhde", ki, vi)
        return state, o

    xs = (jnp.moveaxis(q_inter, 2, 0), jnp.moveaxis(k_inter, 2, 0),
          jnp.moveaxis(vc, 2, 0), jnp.moveaxis(chunk_decay, 2, 0))
    s0 = jnp.zeros((bsz, n_h, dk, dv), jnp.float32)
    _, o_inter = lax.scan(step, s0, xs)
    o = o_intra + jnp.moveaxis(o_inter, 0, 2)
    return o.transpose(0, 2, 3, 1, 4).reshape(bsz, t_len, n_h, dv)


def hybrid_layer(x, norm_w, w_in, gla_w_gate_up, gla_b_gate, gla_norm_w,
                 sc_conv_w, cf_conv_w, cf_conv_b, cf_ln_w, cf_ln_b, w_out):
    bsz, t_len, _ = x.shape
    h = rms_norm(x, norm_w)
    proj = h @ w_in
    (g_q, g_k, g_v, g_gate, g_lr,
     s_b, s_c, s_h, s_gate,
     c_a, c_b, c_gate) = jnp.split(proj, split_points(), axis=-1)

    gate_logits = (g_lr @ gla_w_gate_up + gla_b_gate).astype(jnp.float32)
    log_alpha = jax.nn.log_sigmoid(gate_logits) / GLA_GATE_NORMALIZER
    q = g_q.reshape(bsz, t_len, GLA_HEADS, GLA_DK)
    k = g_k.reshape(bsz, t_len, GLA_HEADS, GLA_DK)
    v = g_v.reshape(bsz, t_len, GLA_HEADS, GLA_DV)
    la = log_alpha.reshape(bsz, t_len, GLA_HEADS, GLA_DK)
    o = gla_chunked(q, k, v, la)
    o = o * lax.rsqrt(jnp.mean(o * o, axis=-1, keepdims=True) + NORM_EPS) * gla_norm_w.astype(jnp.float32)
    o_a = o.reshape(bsz, t_len, GLA_VWIDTH).astype(x.dtype) * jax.nn.silu(g_gate)

    o_b = s_b * causal_depthwise_conv(s_c * s_h, sc_conv_w) * jax.nn.silu(s_gate)

    c = c_a * jax.nn.sigmoid(c_b)
    c = causal_depthwise_conv(c, cf_conv_w, cf_conv_b)
    c = layer_norm(c, cf_ln_w, cf_ln_b)
    o_c = jax.nn.silu(c) * jax.nn.silu(c_gate)

    y = jnp.concatenate([o_a, o_b, o_c], axis=-1) @ w_out
    return x + y


def setup_inputs(seed: int = 0) -> dict:
    key = jax.random.key(seed)
    ks = jax.random.split(key, 14)
    f32 = jnp.float32
    x = jax.random.normal(ks[0], (BATCH, SEQ, D_MODEL), f32)
    norm_w = 1.0 + 0.01 * jax.random.normal(ks[1], (DEPTH, D_MODEL), f32)
    w_in = jax.random.normal(ks[2], (DEPTH, D_MODEL, D_IN), f32) * (D_MODEL ** -0.5)
    gla_w_gate_up = jax.random.normal(ks[3], (DEPTH, GLA_GATE_RANK, GLA_KWIDTH), f32) * (GLA_GATE_RANK ** -0.5)
    gla_b_gate = 0.1 * jax.random.normal(ks[4], (DEPTH, GLA_KWIDTH), f32)
    gla_norm_w = 1.0 + 0.01 * jax.random.normal(ks[5], (DEPTH, GLA_DV), f32)
    sc_conv_w = jax.random.normal(ks[6], (DEPTH, SC_CONV, SC_WIDTH), f32) * (SC_CONV ** -0.5)
    cf_conv_w = jax.random.normal(ks[7], (DEPTH, CF_CONV, CF_WIDTH), f32) * (CF_CONV ** -0.5)
    cf_conv_b = 0.01 * jax.random.normal(ks[8], (DEPTH, CF_WIDTH), f32)
    cf_ln_w = 1.0 + 0.01 * jax.random.normal(ks[9], (DEPTH, CF_WIDTH), f32)
    cf_ln_b = 0.01 * jax.random.normal(ks[10], (DEPTH, CF_WIDTH), f32)
    w_out = jax.random.normal(ks[11], (DEPTH, D_MIX, D_MODEL), f32) * (D_MIX ** -0.5)
    final_norm_w = 1.0 + 0.01 * jax.random.normal(ks[12], (D_MODEL,), f32)
    return {"x": x, "norm_w": norm_w, "w_in": w_in, "gla_w_gate_up": gla_w_gate_up,
            "gla_b_gate": gla_b_gate, "gla_norm_w": gla_norm_w, "sc_conv_w": sc_conv_w,
            "cf_conv_w": cf_conv_w, "cf_conv_b": cf_conv_b, "cf_ln_w": cf_ln_w,
            "cf_ln_b": cf_ln_b, "w_out": w_out, "final_norm_w": final_norm_w}


def reference(x, norm_w, w_in, gla_w_gate_up, gla_b_gate, gla_norm_w, sc_conv_w,
              cf_conv_w, cf_conv_b, cf_ln_w, cf_ln_b, w_out, final_norm_w):
    for layer in range(DEPTH):
        x = hybrid_layer(x, norm_w[layer], w_in[layer], gla_w_gate_up[layer], gla_b_gate[layer],
                         gla_norm_w[layer], sc_conv_w[layer], cf_conv_w[layer], cf_conv_b[layer],
                         cf_ln_w[layer], cf_ln_b[layer], w_out[layer])
    return rms_norm(x, final_norm_w)
```

```python
import dataclasses
import functools

import jax
import jax.numpy as jnp
from jax import lax
from jax.experimental import pallas as pl
from jax.experimental.pallas import tpu as pltpu

NORM_EPS = 1e-6
LN_EPS = 1e-5
GLA_CHUNK = 64
GLA_GATE_NORMALIZER = 16.0

LANES = 128
SUBLANES = 8
VMEM_LIMIT_BYTES = 60 * 1024 * 1024


@dataclasses.dataclass(frozen=True)
class _Cfg:
    d_model: int
    heads: int
    dk: int
    dv: int
    kw: int
    vw: int
    rank: int
    sc: int
    sc_taps: int
    cf: int
    cf_taps: int
    tt: int
    is_last: bool

    @property
    def off_q(self): return 0
    @property
    def off_k(self): return self.kw
    @property
    def off_v(self): return 2 * self.kw
    @property
    def off_g(self): return 2 * self.kw + self.vw
    @property
    def off_lr(self): return 2 * self.kw + 2 * self.vw
    @property
    def gla_cols(self): return self.off_lr + LANES
    @property
    def sc_cols(self): return 4 * self.sc
    @property
    def cf_cols(self): return 3 * self.cf
    @property
    def n_in(self): return self.gla_cols + self.sc_cols + self.cf_cols
    @property
    def d_mix(self): return self.vw + self.sc + self.cf
    @property
    def cf_halo(self): return 32
    @property
    def p_cols(self): return max(self.gla_cols, self.sc_cols, self.cf_cols)


def _sigmoid(x):
    return 1.0 / (1.0 + jnp.exp(-x))


def _silu(x):
    return x * _sigmoid(x)


def _log_sigmoid(z):
    return jnp.minimum(z, 0.0) - jnp.log1p(jnp.exp(-jnp.abs(z)))


def _shifted_taps(win, halo, rows, wtap, n_delay):
    acc = None
    for r in range(SUBLANES):
        if r >= n_delay:
            break
        rolled = win if r == 0 else pltpu.roll(win, r, axis=0)
        a = 0
        while SUBLANES * a + r < n_delay:
            s = SUBLANES * a + r
            lo = halo - SUBLANES * a
            term = rolled[lo:lo + rows, :] * wtap(s)
            acc = term if acc is None else acc + term
            a += 1
    return acc


def _layer_kernel(x_ref, nw_ref, win_ref, wup_ref, bg_ref, gnw_ref, scw_ref, cfw_ref, cfb_ref,
                  lnw_ref, lnb_ref, wout_ref, fnw_ref, o_ref,
                  h_ref, p_ref, mix_ref, s_ref, ubuf_ref, cbuf_ref, *, cfg: _Cfg):
    c = cfg
    tt = c.tt
    f32 = jnp.float32
    bf16 = jnp.bfloat16

    @pl.when(pl.program_id(1) == 0)
    def _reset():
        s_ref[...] = jnp.zeros(s_ref.shape, f32)
        ubuf_ref[0:SUBLANES, :] = jnp.zeros((SUBLANES, c.sc), f32)
        cbuf_ref[0:c.cf_halo, :] = jnp.zeros((c.cf_halo, c.cf), f32)

    rb_n = 128

    def norm_body(r, carry):
        rows = pl.ds(pl.multiple_of(r * rb_n, rb_n), rb_n)
        xb = x_ref[0, rows, :]
        ms = jnp.mean(xb * xb, axis=-1, keepdims=True)
        h_ref[rows, :] = (xb * lax.rsqrt(ms + NORM_EPS) * nw_ref[...]).astype(bf16)
        return carry

    lax.fori_loop(0, tt // rb_n, norm_body, 0)

    p_ref[:, 0:c.gla_cols] = jnp.dot(h_ref[...], win_ref[:, 0:c.gla_cols], preferred_element_type=f32)

    ck = GLA_CHUNK
    row_i = lax.broadcasted_iota(jnp.int32, (ck, ck), 0)
    col_i = lax.broadcasted_iota(jnp.int32, (ck, ck), 1)
    causal = row_i >= col_i
    tril = jnp.where(causal, 1.0, 0.0).astype(bf16)
    q_scale = float(c.dk) ** -0.5

    def gla_body(ci, carry):
        rows = pl.ds(pl.multiple_of(ci * ck, ck), ck)
        lr = p_ref[rows, c.off_lr:c.off_lr + LANES].astype(bf16)
        z = jnp.dot(lr, wup_ref[...], preferred_element_type=f32) + bg_ref[...]
        la = _log_sigmoid(z) * (1.0 / GLA_GATE_NORMALIZER)
        la_hi = la.astype(bf16)
        la_lo = (la - la_hi.astype(f32)).astype(bf16)
        bcum = (jnp.dot(tril, la_hi, preferred_element_type=f32)
                + jnp.dot(tril, la_lo, preferred_element_type=f32))
        for hd in range(c.heads):
            ksl = slice(hd * c.dk, (hd + 1) * c.dk)
            bh = bcum[:, ksl]
            b_mid = bh[ck // 2:ck // 2 + 1, :]
            b_last = bh[ck - 1:ck, :]
            e_fwd = jnp.exp(bh - b_mid)
            e_bwd = jnp.exp(b_mid - bh)
            q = p_ref[rows, c.off_q + hd * c.dk:c.off_q + (hd + 1) * c.dk] * q_scale
            k = p_ref[rows, c.off_k + hd * c.dk:c.off_k + (hd + 1) * c.dk]
            v = p_ref[rows, c.off_v + hd * c.dv:c.off_v + (hd + 1) * c.dv].astype(bf16)
            q_intra = q * e_fwd
            k_intra = k * e_bwd
            scores = lax.dot_general(q_intra.astype(bf16), k_intra.astype(bf16),
                                     (((1,), (1,)), ((), ())), preferred_element_type=f32)
            scores = jnp.where(causal, scores, 0.0)
            q_inter = q_intra * jnp.exp(b_mid)
            k_inter = k_intra * jnp.exp(b_last - b_mid)
            state = s_ref[hd]
            o = (jnp.dot(scores.astype(bf16), v, preferred_element_type=f32)
                 + jnp.dot(q_inter.astype(bf16), state.astype(bf16), preferred_element_type=f32))
            ktv = lax.dot_general(k_inter.astype(bf16), v, (((0,), (0,)), ((), ())),
                                  preferred_element_type=f32)
            decay = jnp.exp(b_last)
            decay_col = jnp.transpose(jnp.broadcast_to(decay, (c.dk, c.dk)))
            for j in range(c.dv // c.dk):
                vs = slice(j * c.dk, (j + 1) * c.dk)
                s_ref[hd, :, vs] = state[:, vs] * decay_col + ktv[:, vs]
            ms = jnp.mean(o * o, axis=-1, keepdims=True)
            on = o * lax.rsqrt(ms + NORM_EPS) * gnw_ref[...]
            g = p_ref[rows, c.off_g + hd * c.dv:c.off_g + (hd + 1) * c.dv]
            mix_ref[rows, hd * c.dv:(hd + 1) * c.dv] = (on * _silu(g)).astype(bf16)
        return carry

    lax.fori_loop(0, tt // ck, gla_body, 0)

    o0 = c.gla_cols
    p_ref[:, 0:c.sc_cols] = jnp.dot(h_ref[...], win_ref[:, o0:o0 + c.sc_cols], preferred_element_type=f32)
    rb = 64

    def sc_fill(r, carry):
        rows = pl.ds(pl.multiple_of(r * rb, rb), rb)
        dst = pl.ds(pl.multiple_of(r * rb + SUBLANES, SUBLANES), rb)
        ubuf_ref[dst, :] = p_ref[rows, c.sc:2 * c.sc] * p_ref[rows, 2 * c.sc:3 * c.sc]
        return carry

    lax.fori_loop(0, tt // rb, sc_fill, 0)

    def sc_body(r, carry):
        base = pl.multiple_of(r * rb, rb)
        rows = pl.ds(base, rb)
        for lt in range(c.sc // LANES):
            ls = slice(lt * LANES, (lt + 1) * LANES)
            win = ubuf_ref[pl.ds(base, rb + SUBLANES), ls]
            conv = _shifted_taps(win, SUBLANES, rb, lambda s: scw_ref[s:s + 1, ls], c.sc_taps)
            sb = p_ref[rows, lt * LANES:(lt + 1) * LANES]
            sg = p_ref[rows, 3 * c.sc + lt * LANES:3 * c.sc + (lt + 1) * LANES]
            mix_ref[rows, c.vw + lt * LANES:c.vw + (lt + 1) * LANES] = (sb * conv * _silu(sg)).astype(bf16)
        return carry

    lax.fori_loop(0, tt // rb, sc_body, 0)
    ubuf_ref[0:SUBLANES, :] = ubuf_ref[tt:tt + SUBLANES, :]

    o1 = c.gla_cols + c.sc_cols
    p_ref[:, 0:c.cf_cols] = jnp.dot(h_ref[...], win_ref[:, o1:o1 + c.cf_cols], preferred_element_type=f32)
    halo = c.cf_halo

    def cf_fill(r, carry):
        rows = pl.ds(pl.multiple_of(r * rb, rb), rb)
        dst = pl.ds(pl.multiple_of(r * rb + halo, SUBLANES), rb)
        cbuf_ref[dst, :] = p_ref[rows, 0:c.cf] * _sigmoid(p_ref[rows, c.cf:2 * c.cf])
        return carry

    lax.fori_loop(0, tt // rb, cf_fill, 0)

    n_lt = c.cf // LANES

    def cf_body(r, carry):
        base = pl.multiple_of(r * rb, rb)
        rows = pl.ds(base, rb)
        ys = []
        for lt in range(n_lt):
            ls = slice(lt * LANES, (lt + 1) * LANES)
            win = cbuf_ref[pl.ds(base, rb + halo), ls]
            y = _shifted_taps(win, halo, rb, lambda s: cfw_ref[s:s + 1, ls], c.cf_taps)
            ys.append(y + cfb_ref[:, ls])
        tot = ys[0]
        for y in ys[1:]:
            tot = tot + y
        mu = jnp.sum(tot, axis=-1, keepdims=True) * (1.0 / c.cf)
        yc = [y - mu for y in ys]
        sq = yc[0] * yc[0]
        for y in yc[1:]:
            sq = sq + y * y
        var = jnp.sum(sq, axis=-1, keepdims=True) * (1.0 / c.cf)
        inv = lax.rsqrt(var + LN_EPS)
        for lt in range(n_lt):
            ls = slice(lt * LANES, (lt + 1) * LANES)
            yn = yc[lt] * inv * lnw_ref[:, ls] + lnb_ref[:, ls]
            cg = p_ref[rows, 2 * c.cf + lt * LANES:2 * c.cf + (lt + 1) * LANES]
            mix_ref[rows, c.vw + c.sc + lt * LANES:c.vw + c.sc + (lt + 1) * LANES] = (
                _silu(yn) * _silu(cg)).astype(bf16)
        return carry

    lax.fori_loop(0, tt // rb, cf_body, 0)
    cbuf_ref[0:halo, :] = cbuf_ref[tt:tt + halo, :]

    y = jnp.dot(mix_ref[...], wout_ref[...], preferred_element_type=f32)
    out = x_ref[0] + y
    if c.is_last:
        ms = jnp.mean(out * out, axis=-1, keepdims=True)
        out = out * lax.rsqrt(ms + NORM_EPS) * fnw_ref[...]
    o_ref[0] = out


def _pick_time_tile(t_len):
    for tt in (512, 256, 128, 64):
        if t_len % tt == 0:
            return tt
    raise ValueError(f"sequence length {t_len} must be a multiple of {GLA_CHUNK}")


def _const_spec(shape, layer):
    nd = len(shape)
    return pl.BlockSpec((None,) + tuple(shape), lambda b, i: (layer,) + (0,) * nd,
                        pipeline_mode=pl.Buffered(1))


def kernel(x, norm_w, w_in, gla_w_gate_up, gla_b_gate, gla_norm_w, sc_conv_w, cf_conv_w, cf_conv_b,
           cf_ln_w, cf_ln_b, w_out, final_norm_w):
    bsz, t_len, d_model = x.shape
    depth = w_in.shape[0]
    rank, kw = gla_w_gate_up.shape[1:]
    dv = gla_norm_w.shape[-1]
    sc_taps, sc = sc_conv_w.shape[1:]
    cf_taps, cf = cf_conv_w.shape[1:]
    vw = w_out.shape[1] - sc - cf
    heads = vw // dv
    dk = kw // heads
    assert w_in.shape[-1] == 2 * kw + 2 * vw + rank + 4 * sc + 3 * cf
    assert dk == LANES and dv % LANES == 0 and sc % LANES == 0 and cf % LANES == 0
    assert rank <= LANES and sc_taps <= SUBLANES and cf_taps <= 32
    tt = _pick_time_tile(t_len)

    f32, bf16 = jnp.float32, jnp.bfloat16
    n_gla = 2 * kw + 2 * vw
    w_lr = jnp.pad(w_in[:, :, n_gla:n_gla + rank], ((0, 0), (0, 0), (0, LANES - rank)))
    w_in_r = jnp.concatenate([w_in[:, :, :n_gla], w_lr, w_in[:, :, n_gla + rank:]], axis=-1).astype(bf16)
    w_up = jnp.pad(gla_w_gate_up, ((0, 0), (0, LANES - rank), (0, 0))).astype(bf16)
    w_out_b = w_out.astype(bf16)
    sc_w = jnp.pad(sc_conv_w[:, ::-1, :], ((0, 0), (0, SUBLANES - sc_taps), (0, 0)))
    cf_w = jnp.pad(cf_conv_w[:, ::-1, :], ((0, 0), (0, 32 - cf_taps), (0, 0)))
    row = lambda a: a.reshape(a.shape[0], 1, a.shape[-1])
    fnw = final_norm_w.reshape(1, 1, d_model)

    for layer in range(depth):
        cfg = _Cfg(d_model=d_model, heads=heads, dk=dk, dv=dv, kw=kw, vw=vw, rank=rank, sc=sc,
                   sc_taps=sc_taps, cf=cf, cf_taps=cf_taps, tt=tt, is_last=(layer == depth - 1))
        x_spec = pl.BlockSpec((1, tt, d_model), lambda b, i: (b, i, 0))
        x = pl.pallas_call(
            functools.partial(_layer_kernel, cfg=cfg),
            grid=(bsz, t_len // tt),
            in_specs=[
                x_spec,
                _const_spec((1, d_model), layer),
                _const_spec((d_model, cfg.n_in), layer),
                _const_spec((LANES, kw), layer),
                _const_spec((1, kw), layer),
                _const_spec((1, dv), layer),
                _const_spec((SUBLANES, sc), layer),
                _const_spec((32, cf), layer),
                _const_spec((1, cf), layer),
                _const_spec((1, cf), layer),
                _const_spec((1, cf), layer),
                _const_spec((cfg.d_mix, d_model), layer),
                _const_spec((1, d_model), 0),
            ],
            out_specs=x_spec,
            out_shape=jax.ShapeDtypeStruct(x.shape, f32),
            scratch_shapes=[
                pltpu.VMEM((tt, d_model), bf16),
                pltpu.VMEM((tt, cfg.p_cols), f32),
                pltpu.VMEM((tt, cfg.d_mix), bf16),
                pltpu.VMEM((heads, dk, dv), f32),
                pltpu.VMEM((tt + SUBLANES, sc), f32),
                pltpu.VMEM((tt + cfg.cf_halo, cf), f32),
            ],
            compiler_params=pltpu.CompilerParams(
                dimension_semantics=("arbitrary", "arbitrary"),
                vmem_limit_bytes=VMEM_LIMIT_BYTES),
            name=f"hybrid_layer_{layer}",
        )(x, row(norm_w), w_in_r, w_up, row(gla_b_gate), row(gla_norm_w), sc_w, cf_w,
          row(cf_conv_b), row(cf_ln_w), row(cf_ln_b), w_out_b, fnw)
    return x
```

```python
import dataclasses
import functools

import jax
import jax.numpy as jnp
from jax import lax
from jax.experimental import pallas as pl
from jax.experimental.pallas import tpu as pltpu

NORM_EPS = 1e-6
LN_EPS = 1e-5
GLA_CHUNK = 64
GLA_GATE_NORMALIZER = 16.0

LANES = 128
SUBLANES = 8
MXU_COLS = 256
VMEM_LIMIT_BYTES = 60 * 1024 * 1024
CONV_ROWS = 64
CF_HALO = 32


@dataclasses.dataclass(frozen=True)
class _Cfg:
    d_model: int
    heads: int
    dk: int
    dv: int
    kw: int
    vw: int
    sc: int
    sc_taps: int
    cf: int
    cf_taps: int
    tt: int
    is_last: bool

    @property
    def gla_cols(self): return 2 * self.kw + 2 * self.vw + MXU_COLS
    @property
    def sc_cols(self): return 4 * self.sc
    @property
    def cf_cols(self): return 3 * self.cf
    @property
    def gla_blocks(self): return self.gla_cols // MXU_COLS
    @property
    def sc_blocks(self): return self.sc_cols // MXU_COLS
    @property
    def cf_blocks(self): return self.cf_cols // MXU_COLS
    @property
    def n_blocks(self): return self.gla_blocks + self.sc_blocks + self.cf_blocks
    @property
    def d_mix(self): return self.vw + self.sc + self.cf


def _sigmoid(x):
    return 1.0 / (1.0 + jnp.exp(-x))


def _silu(x):
    return x * _sigmoid(x)


def _log_sigmoid(z):
    return jnp.minimum(z, 0.0) - jnp.log1p(jnp.exp(-jnp.abs(z)))


def _shifted_taps(win, halo, rows, wtap, n_delay):
    acc = None
    for r in range(min(SUBLANES, n_delay)):
        rolled = win if r == 0 else pltpu.roll(win, r, axis=0)
        for a in range((n_delay - r + SUBLANES - 1) // SUBLANES):
            lo = halo - SUBLANES * a
            term = rolled[lo:lo + rows, :] * wtap(SUBLANES * a + r)
            acc = term if acc is None else acc + term
    return acc


def _spread(n_items, n_iters):
    lo, extra = divmod(n_items, n_iters)
    segs, item = [], 0
    if extra:
        segs.append((0, extra, lo + 1, item))
        item += extra * (lo + 1)
    segs.append((extra, n_iters, lo, item))
    return segs


def _layer_kernel(x_ref, xn_ref, nw_ref, win_ref, wup_ref, bg_ref, gnw_ref, scw_ref, cfw_ref, cfb_ref,
                  lnw_ref, lnb_ref, wout_ref, fnw_ref, o_ref,
                  h_ref, p_ref, mix_ref, s_ref, sco_ref, qin_ref, kin_ref, dec_ref, ubuf_ref, cbuf_ref,
                  *, cfg: _Cfg):
    c = cfg
    tt = c.tt
    f32 = jnp.float32
    bf16 = jnp.bfloat16
    step = pl.program_id(0) * pl.num_programs(1) + pl.program_id(1)
    cur = step % 2
    nxt = 1 - cur

    def pcol(group_block0, off, width):
        blk, lo = divmod(off, MXU_COLS)
        assert lo + width <= MXU_COLS
        return group_block0 + blk, slice(lo, lo + width)

    def proj_block(slot, j):
        p_ref[j] = jnp.dot(h_ref[slot], win_ref[j], preferred_element_type=f32)

    def norm_rows(src_ref, slot, rows):
        xb = src_ref[0, rows, :]
        ms = jnp.mean(xb * xb, axis=-1, keepdims=True)
        h_ref[slot, rows, :] = (xb * lax.rsqrt(ms + NORM_EPS) * nw_ref[...]).astype(bf16)

    @pl.when(step == 0)
    def _prologue():
        def body(r, carry):
            norm_rows(x_ref, 0, pl.ds(pl.multiple_of(r * 128, 128), 128))
            return carry
        lax.fori_loop(0, tt // 128, body, 0)
        for j in range(c.gla_blocks):
            proj_block(0, j)

    @pl.when(pl.program_id(1) == 0)
    def _reset():
        s_ref[...] = jnp.zeros(s_ref.shape, f32)
        ubuf_ref[0:SUBLANES, :] = jnp.zeros((SUBLANES, c.sc), f32)
        cbuf_ref[0:CF_HALO, :] = jnp.zeros((CF_HALO, c.cf), f32)

    ck = GLA_CHUNK
    row_i = lax.broadcasted_iota(jnp.int32, (ck, ck), 0)
    col_i = lax.broadcasted_iota(jnp.int32, (ck, ck), 1)
    causal = row_i >= col_i
    tril = jnp.where(causal, 1.0, 0.0).astype(bf16)
    q_scale = float(c.dk) ** -0.5
    off_q, off_k, off_v, off_g, off_lr = 0, c.kw, 2 * c.kw, 2 * c.kw + c.vw, 2 * c.kw + 2 * c.vw

    def chunk_rows(ci):
        return pl.ds(pl.multiple_of(ci * ck, ck), ck)

    def pa(rows, off, width):
        blk, ls = pcol(0, off, width)
        return p_ref[blk, rows, ls]

    def gla_gates(ci):
        rows = chunk_rows(ci)
        lr = pa(rows, off_lr, LANES).astype(bf16)
        z = jnp.dot(lr, wup_ref[...], preferred_element_type=f32) + bg_ref[...]
        la =_log_sigmoid(z) * (1.0 / GLA_GATE_NORMALIZER)
        la_hi = la.astype(bf16)
        la_lo = (la - la_hi.astype(f32)).astype(bf16)
        bcum = (jnp.dot(tril, la_hi, preferred_element_type=f32)
                + jnp.dot(tril, la_lo, preferred_element_type=f32))
        b_last_all = bcum[ck - 1:ck, :]
        dec_ref[pl.ds(pl.multiple_of(ci * SUBLANES, SUBLANES), SUBLANES), :] = jnp.broadcast_to(
            jnp.exp(b_last_all), (SUBLANES, c.kw))
        for hd in range(c.heads):
            ks = slice(hd * c.dk, (hd + 1) * c.dk)
            bh = bcum[:, ks]
            b_mid = bh[ck // 2:ck // 2 + 1, :]
            b_last = bh[ck - 1:ck, :]
            q_intra = pa(rows, off_q + hd * c.dk, c.dk) * q_scale * jnp.exp(bh - b_mid)
            k_intra = pa(rows, off_k + hd * c.dk, c.dk) * jnp.exp(b_mid - bh)
            scores = lax.dot_general(q_intra.astype(bf16), k_intra.astype(bf16),
                                     (((1,), (1,)), ((), ())), preferred_element_type=f32)
            sco_ref[rows, hd * LANES:hd * LANES + ck] = jnp.where(causal, scores, 0.0).astype(bf16)
            qin_ref[rows, ks] = (q_intra * jnp.exp(b_mid)).astype(bf16)
            kin_ref[rows, ks] = (k_intra * jnp.exp(b_last - b_mid)).astype(bf16)

    def gla_state(ci):
        rows = chunk_rows(ci)
        for hd in range(c.heads):
            ks = slice(hd * c.dk, (hd + 1) * c.dk)
            v = pa(rows, off_v + hd * c.dv, c.dv).astype(bf16)
            state = s_ref[hd]
            o = (jnp.dot(sco_ref[rows, hd * LANES:hd * LANES + ck], v, preferred_element_type=f32)
                 + jnp.dot(qin_ref[rows, ks], state.astype(bf16), preferred_element_type=f32))
            ktv = lax.dot_general(kin_ref[rows, ks], v, (((0,), (0,)), ((), ())),
                                  preferred_element_type=f32)
            decay = dec_ref[pl.ds(pl.multiple_of(ci * SUBLANES, SUBLANES), 1), ks]
            decay_col = jnp.transpose(jnp.broadcast_to(decay, (c.dk, c.dk)))
            for j in range(c.dv // c.dk):
                vs = slice(j * c.dk, (j + 1) * c.dk)
                s_ref[hd, :, vs] = state[:, vs] * decay_col + ktv[:, vs]
            ms = jnp.mean(o * o, axis=-1, keepdims=True)
            on = o * lax.rsqrt(ms + NORM_EPS) * gnw_ref[...]
            g = pa(rows, off_g + hd * c.dv, c.dv)
            mix_ref[rows, hd * c.dv:(hd + 1) * c.dv] = (on * _silu(g)).astype(bf16)

    n_chunks = tt // ck
    gla_gates(0)
    for t0, t1, per_trip, first in _spread(c.sc_blocks + c.cf_blocks, n_chunks - 1):
        def gla_body(ci, carry, t0=t0, per_trip=per_trip, first=first):
            gla_state(ci)
            gla_gates(ci + 1)
            norm_rows(xn_ref, nxt, chunk_rows(ci))
            for t in range(per_trip):
                proj_block(cur, c.gla_blocks + first + (ci - t0) * per_trip + t)
            return carry
        lax.fori_loop(t0, t1, gla_body, 0)
    gla_state(n_chunks - 1)
    norm_rows(xn_ref, nxt, chunk_rows(n_chunks - 1))

    rb = CONV_ROWS
    blk_b = c.gla_blocks

    def sc_fill(r, carry):
        rows = pl.ds(pl.multiple_of(r * rb, rb), rb)
        dst = pl.ds(pl.multiple_of(r * rb + SUBLANES, SUBLANES), rb)
        for lt in range(c.sc // LANES):
            b1, l1 = pcol(blk_b, c.sc + lt * LANES, LANES)
            b2, l2 = pcol(blk_b, 2 * c.sc + lt * LANES, LANES)
            ubuf_ref[dst, lt * LANES:(lt + 1) * LANES] = p_ref[b1, rows, l1] * p_ref[b2, rows, l2]
        return carry

    lax.fori_loop(0, tt // rb, sc_fill, 0)

    def sc_body(r, carry):
        base = pl.multiple_of(r * rb, rb)
        rows = pl.ds(base, rb)
        for lt in range(c.sc // LANES):
            ls = slice(lt * LANES, (lt + 1) * LANES)
            win = ubuf_ref[pl.ds(base, rb + SUBLANES), ls]
            conv = _shifted_taps(win, SUBLANES, rb, lambda s: scw_ref[s:s + 1, ls], c.sc_taps)
            b0, l0 = pcol(blk_b, lt * LANES, LANES)
            b3, l3 = pcol(blk_b, 3 * c.sc + lt * LANES, LANES)
            mix_ref[rows, c.vw + lt * LANES:c.vw + (lt + 1) * LANES] = (
                p_ref[b0, rows, l0] * conv * _silu(p_ref[b3, rows, l3])).astype(bf16)
        return carry

    lax.fori_loop(0, tt // rb, sc_body, 0)
    ubuf_ref[0:SUBLANES, :] = ubuf_ref[tt:tt + SUBLANES, :]

    blk_c = c.gla_blocks + c.sc_blocks
    n_lt = c.cf // LANES

    def cf_fill(r, carry):
        rows = pl.ds(pl.multiple_of(r * rb, rb), rb)
        dst = pl.ds(pl.multiple_of(r * rb + CF_HALO, SUBLANES), rb)
        for lt in range(n_lt):
            b0, l0 = pcol(blk_c, lt * LANES, LANES)
            b1, l1 = pcol(blk_c, c.cf + lt * LANES, LANES)
            cbuf_ref[dst, lt * LANES:(lt + 1) * LANES] = p_ref[b0, rows, l0] * _sigmoid(p_ref[b1, rows, l1])
        return carry

    lax.fori_loop(0, tt // rb, cf_fill, 0)

    def cf_rows(r):
        base = pl.multiple_of(r * rb, rb)
        rows = pl.ds(base, rb)
        ys = []
        for lt in range(n_lt):
            ls = slice(lt * LANES, (lt + 1) * LANES)
            win = cbuf_ref[pl.ds(base, rb + CF_HALO), ls]
            y = _shifted_taps(win, CF_HALO, rb, lambda s: cfw_ref[s:s + 1, ls], c.cf_taps)
            ys.append(y + cfb_ref[:, ls])
        tot = ys[0]
        for y in ys[1:]:
            tot = tot + y
        mu = jnp.sum(tot, axis=-1, keepdims=True) * (1.0 / c.cf)
        yc = [y - mu for y in ys]
        sq = yc[0] * yc[0]
        for y in yc[1:]:
            sq = sq + y * y
        var = jnp.sum(sq, axis=-1, keepdims=True) * (1.0 / c.cf)
        inv = lax.rsqrt(var + LN_EPS)
        for lt in range(n_lt):
            ls = slice(lt * LANES, (lt + 1) * LANES)
            yn = yc[lt] * inv * lnw_ref[:, ls] + lnb_ref[:, ls]
            b2, l2 = pcol(blk_c, 2 * c.cf + lt * LANES, LANES)
            mix_ref[rows, c.vw + c.sc + lt * LANES:c.vw + c.sc + (lt + 1) * LANES] = (
                _silu(yn) * _silu(p_ref[b2, rows, l2])).astype(bf16)

    for t0, t1, per_trip, first in _spread(c.gla_blocks, tt // rb):
        def cf_body(r, carry, t0=t0, per_trip=per_trip, first=first):
            cf_rows(r)
            for t in range(per_trip):
                proj_block(nxt, first + (r - t0) * per_trip + t)
            return carry
        lax.fori_loop(t0, t1, cf_body, 0)
    cbuf_ref[0:CF_HALO, :] = cbuf_ref[tt:tt + CF_HALO, :]

    y = jnp.dot(mix_ref[...], wout_ref[...], preferred_element_type=f32)
    out = x_ref[0] + y
    if c.is_last:
        ms = jnp.mean(out * out, axis=-1, keepdims=True)
        out = out * lax.rsqrt(ms + NORM_EPS) * fnw_ref[...]
    o_ref[0] = out


def _pick_time_tile(t_len):
    for tt in (512, 256, 128):
        if t_len % tt == 0:
            return tt
    raise ValueError(f"sequence length {t_len} must be a multiple of 128")


def _const_spec(shape, layer):
    nd = len(shape)
    return pl.BlockSpec((None,) + tuple(shape), lambda b, i: (layer,) + (0,) * nd,
                        pipeline_mode=pl.Buffered(1))


def kernel(x, norm_w, w_in, gla_w_gate_up, gla_b_gate, gla_norm_w, sc_conv_w, cf_conv_w, cf_conv_b,
           cf_ln_w, cf_ln_b, w_out, final_norm_w):
    bsz, t_len, d_model = x.shape
    depth = w_in.shape[0]
    rank, kw = gla_w_gate_up.shape[1:]
    dv = gla_norm_w.shape[-1]
    sc_taps, sc = sc_conv_w.shape[1:]
    cf_taps, cf = cf_conv_w.shape[1:]
    vw = w_out.shape[1] - sc - cf
    heads = vw // dv
    dk = kw // heads
    assert w_in.shape[-1] == 2 * kw + 2 * vw + rank + 4 * sc + 3 * cf
    assert dk == LANES and dv % LANES == 0 and dv <= MXU_COLS and MXU_COLS % dv == 0
    assert sc % MXU_COLS == 0 and cf % MXU_COLS == 0 and kw % MXU_COLS == 0 and vw % MXU_COLS == 0
    assert rank <= LANES and sc_taps <= SUBLANES and cf_taps <= CF_HALO
    tt = _pick_time_tile(t_len)
    n_tiles = t_len // tt

    f32, bf16 = jnp.float32, jnp.bfloat16
    n_gla = 2 * kw + 2 * vw
    w_lr = jnp.pad(w_in[:, :, n_gla:n_gla + rank], ((0, 0), (0, 0), (0, MXU_COLS - rank)))
    w_in_r = jnp.concatenate([w_in[:, :, :n_gla], w_lr, w_in[:, :, n_gla + rank:]], axis=-1).astype(bf16)
    n_blocks = w_in_r.shape[-1] // MXU_COLS
    w_in_r = w_in_r.reshape(depth, d_model, n_blocks, MXU_COLS).transpose(0, 2, 1, 3)
    w_up = jnp.pad(gla_w_gate_up, ((0, 0), (0, LANES - rank), (0, 0))).astype(bf16)
    w_out_b = w_out.astype(bf16)
    sc_w = jnp.pad(sc_conv_w[:, ::-1, :], ((0, 0), (0, SUBLANES - sc_taps), (0, 0)))
    cf_w = jnp.pad(cf_conv_w[:, ::-1, :], ((0, 0), (0, CF_HALO - cf_taps), (0, 0)))
    row = lambda a: a.reshape(a.shape[0], 1, a.shape[-1])
    fnw = final_norm_w.reshape(1, 1, d_model)

    def next_tile(b, i):
        flat = jnp.minimum(b * n_tiles + i + 1, bsz * n_tiles - 1)
        return flat // n_tiles, flat % n_tiles, 0

    for layer in range(depth):
        cfg = _Cfg(d_model=d_model, heads=heads, dk=dk, dv=dv, kw=kw, vw=vw, sc=sc,
                   sc_taps=sc_taps, cf=cf, cf_taps=cf_taps, tt=tt, is_last=(layer == depth - 1))
        assert cfg.n_blocks == n_blocks
        x_spec = pl.BlockSpec((1, tt, d_model), lambda b, i: (b, i, 0))
        x = pl.pallas_call(
            functools.partial(_layer_kernel, cfg=cfg),
            grid=(bsz, n_tiles),
            in_specs=[
                x_spec,
                pl.BlockSpec((1, tt, d_model), next_tile),
                _const_spec((1, d_model), layer),
                _const_spec((n_blocks, d_model, MXU_COLS), layer),
                _const_spec((LANES, kw), layer),
                _const_spec((1, kw), layer),
                _const_spec((1, dv), layer),
                _const_spec((SUBLANES, sc), layer),
                _const_spec((CF_HALO, cf), layer),
                _const_spec((1, cf), layer),
                _const_spec((1, cf), layer),
                _const_spec((1, cf), layer),
                _const_spec((cfg.d_mix, d_model), layer),
                _const_spec((1, d_model), 0),
            ],
            out_specs=x_spec,
            out_shape=jax.ShapeDtypeStruct(x.shape, f32),
            scratch_shapes=[
                pltpu.VMEM((2, tt, d_model), bf16),
                pltpu.VMEM((n_blocks, tt, MXU_COLS), f32),
                pltpu.VMEM((tt, cfg.d_mix), bf16),
                pltpu.VMEM((heads, dk, dv), f32),
                pltpu.VMEM((tt, heads * LANES), bf16),
                pltpu.VMEM((tt, kw), bf16),
                pltpu.VMEM((tt, kw), bf16),
                pltpu.VMEM((tt // GLA_CHUNK * SUBLANES, kw), f32),
                pltpu.VMEM((tt + SUBLANES, sc), f32),
                pltpu.VMEM((tt + CF_HALO, cf), f32),
            ],
            compiler_params=pltpu.CompilerParams(
                dimension_semantics=("arbitrary", "arbitrary"),
                vmem_limit_bytes=VMEM_LIMIT_BYTES),
            name=f"hybrid_layer_{layer}",
        )(x, x, row(norm_w), w_in_r, w_up, row(gla_b_gate), row(gla_norm_w), sc_w, cf_w,
          row(cf_conv_b), row(cf_ln_w), row(cf_ln_b), w_out_b, fnw)
    return x
```

```python
import dataclasses
import functools

import jax
import jax.numpy as jnp
from jax import lax
from jax.experimental import pallas as pl
from jax.experimental.pallas import tpu as pltpu

NORM_EPS = 1e-6
LN_EPS = 1e-5
GLA_CHUNK = 64
GLA_GATE_NORMALIZER = 16.0

LANES = 128
SUBLANES = 8
MXU_COLS = 256
VMEM_LIMIT_BYTES = 60 * 1024 * 1024
CONV_ROWS = 64
CF_HALO = 32


@dataclasses.dataclass(frozen=True)
class _Cfg:
    d_model: int
    heads: int
    dk: int
    dv: int
    kw: int
    vw: int
    sc: int
    sc_taps: int
    cf: int
    cf_taps: int
    tt: int
    is_last: bool

    @property
    def gla_cols(self): return 2 * self.kw + 2 * self.vw + MXU_COLS
    @property
    def sc_cols(self): return 4 * self.sc
    @property
    def cf_cols(self): return 3 * self.cf
    @property
    def gla_blocks(self): return self.gla_cols // MXU_COLS
    @property
    def sc_blocks(self): return self.sc_cols // MXU_COLS
    @property
    def cf_blocks(self): return self.cf_cols // MXU_COLS
    @property
    def n_blocks(self): return self.gla_blocks + self.sc_blocks + self.cf_blocks
    @property
    def d_mix(self): return self.vw + self.sc + self.cf


def _sigmoid(x):
    return 1.0 / (1.0 + jnp.exp(-x))


def _silu(x):
    return x * _sigmoid(x)


def _log_sigmoid(z):
    return jnp.minimum(z, 0.0) - jnp.log1p(jnp.exp(-jnp.abs(z)))


def _shifted_taps(win, halo, rows, wtap, n_delay):
    acc = None
    for r in range(min(SUBLANES, n_delay)):
        rolled = win if r == 0 else pltpu.roll(win, r, axis=0)
        for a in range((n_delay - r + SUBLANES - 1) // SUBLANES):
            lo = halo - SUBLANES * a
            term = rolled[lo:lo + rows, :] * wtap(SUBLANES * a + r)
            acc = term if acc is None else acc + term
    return acc


def _spread(n_items, n_iters):
    lo, extra = divmod(n_items, n_iters)
    segs, item = [], 0
    if extra:
        segs.append((0, extra, lo + 1, item))
        item += extra * (lo + 1)
    segs.append((extra, n_iters, lo, item))
    return segs


def _layer_kernel(x_ref, xn_ref, nw_ref, win_ref, wup_ref, bg_ref, gnw_ref, scw_ref, cfw_ref, cfb_ref,
                  lnw_ref, lnb_ref, wout_ref, fnw_ref, o_ref,
                  h_ref, p_ref, mix_ref, s_ref, sco_ref, qin_ref, kin_ref, dec_ref, ubuf_ref, cbuf_ref,
                  *, cfg: _Cfg):
    c = cfg
    tt = c.tt
    f32 = jnp.float32
    bf16 = jnp.bfloat16
    step = pl.program_id(0) * pl.num_programs(1) + pl.program_id(1)
    cur = step % 2
    nxt = 1 - cur

    def pcol(group_block0, off, width):
        blk, lo = divmod(off, MXU_COLS)
        assert lo + width <= MXU_COLS
        return group_block0 + blk, slice(lo, lo + width)

    def proj_block(slot, j):
        p_ref[j] = jnp.dot(h_ref[slot], win_ref[j], preferred_element_type=f32)

    def proj_half(slot, j, half):
        rows = slice(half * (tt // 2), (half + 1) * (tt // 2))
        p_ref[j, rows, :] = jnp.dot(h_ref[slot, rows, :], win_ref[j], preferred_element_type=f32)

    def norm_rows(src_ref, slot, rows):
        xb = src_ref[0, rows, :]
        ms = jnp.mean(xb * xb, axis=-1, keepdims=True)
        h_ref[slot, rows, :] = (xb * lax.rsqrt(ms + NORM_EPS) * nw_ref[...]).astype(bf16)

    @pl.when(step == 0)
    def _prologue():
        def body(r, carry):
            norm_rows(x_ref, 0, pl.ds(pl.multiple_of(r * 128, 128), 128))
            return carry
        lax.fori_loop(0, tt // 128, body, 0)
        for j in range(c.gla_blocks):
            proj_block(0, j)

    @pl.when(pl.program_id(1) == 0)
    def _reset():
        s_ref[...] = jnp.zeros(s_ref.shape, f32)
        ubuf_ref[0:SUBLANES, :] = jnp.zeros((SUBLANES, c.sc), f32)
        cbuf_ref[0:CF_HALO, :] = jnp.zeros((CF_HALO, c.cf), f32)

    ck = GLA_CHUNK
    row_i = lax.broadcasted_iota(jnp.int32, (ck, ck), 0)
    col_i = lax.broadcasted_iota(jnp.int32, (ck, ck), 1)
    causal = row_i >= col_i
    tril = jnp.where(causal, 1.0, 0.0).astype(bf16)
    q_scale = float(c.dk) ** -0.5
    off_q, off_k, off_v, off_g, off_lr = 0, c.kw, 2 * c.kw, 2 * c.kw + c.vw, 2 * c.kw + 2 * c.vw

    def chunk_rows(ci):
        return pl.ds(pl.multiple_of(ci * ck, ck), ck)

    def pa(rows, off, width):
        blk, ls = pcol(0, off, width)
        return p_ref[blk, rows, ls]

    def gate_logits(ci):
        lr = pa(chunk_rows(ci), off_lr, LANES).astype(bf16)
        return jnp.dot(lr, wup_ref[...], preferred_element_type=f32) + bg_ref[...]

    def gate_cumsum(z):
        la = _log_sigmoid(z) * (1.0 / GLA_GATE_NORMALIZER)
        la_hi = la.astype(bf16)
        la_lo = (la - la_hi.astype(f32)).astype(bf16)
        return (jnp.dot(tril, la_hi, preferred_element_type=f32)
                + jnp.dot(tril, la_lo, preferred_element_type=f32))

    def gate_scores(ci, bcum):
        rows = chunk_rows(ci)
        b_last_all = bcum[ck - 1:ck, :]
        dec_ref[pl.ds(pl.multiple_of(ci * SUBLANES, SUBLANES), SUBLANES), :] = jnp.broadcast_to(
            jnp.exp(b_last_all), (SUBLANES, c.kw))
        for hd in range(c.heads):
            ks = slice(hd * c.dk, (hd + 1) * c.dk)
            bh = bcum[:, ks]
            b_mid = bh[ck // 2:ck // 2 + 1, :]
            b_last = bh[ck - 1:ck, :]
            q_intra = pa(rows, off_q + hd * c.dk, c.dk) * q_scale * jnp.exp(bh - b_mid)
            k_intra = pa(rows, off_k + hd * c.dk, c.dk) * jnp.exp(b_mid - bh)
            scores = lax.dot_general(q_intra.astype(bf16), k_intra.astype(bf16),
                                     (((1,), (1,)), ((), ())), preferred_element_type=f32)
            sco_ref[rows, hd * LANES:hd * LANES + ck] = jnp.where(causal, scores, 0.0).astype(bf16)
            qin_ref[rows, ks] = (q_intra * jnp.exp(b_mid)).astype(bf16)
            kin_ref[rows, ks] = (k_intra * jnp.exp(b_last - b_mid)).astype(bf16)

    def gla_state(ci):
        rows = chunk_rows(ci)
        for hd in range(c.heads):
            ks = slice(hd * c.dk, (hd + 1) * c.dk)
            v = pa(rows, off_v + hd * c.dv, c.dv).astype(bf16)
            state = s_ref[hd]
            o = (jnp.dot(sco_ref[rows, hd * LANES:hd * LANES + ck], v, preferred_element_type=f32)
                 + jnp.dot(qin_ref[rows, ks], state.astype(bf16), preferred_element_type=f32))
            ktv = lax.dot_general(kin_ref[rows, ks], v, (((0,), (0,)), ((), ())),
                                  preferred_element_type=f32)
            decay = dec_ref[pl.ds(pl.multiple_of(ci * SUBLANES, SUBLANES), 1), ks]
            decay_col = jnp.transpose(jnp.broadcast_to(decay, (c.dk, c.dk)))
            for j in range(c.dv // c.dk):
                vs = slice(j * c.dk, (j + 1) * c.dk)
                s_ref[hd, :, vs] = state[:, vs] * decay_col + ktv[:, vs]
            ms = jnp.mean(o * o, axis=-1, keepdims=True)
            on = o * lax.rsqrt(ms + NORM_EPS) * gnw_ref[...]
            g = pa(rows, off_g + hd * c.dv, c.dv)
            mix_ref[rows, hd * c.dv:(hd + 1) * c.dv] = (on * _silu(g)).astype(bf16)

    n_chunks = tt // ck
    gate_scores(0, gate_cumsum(gate_logits(0)))
    for t0, t1, per_trip, first in _spread(c.sc_blocks + c.cf_blocks, n_chunks - 1):
        def gla_body(ci, carry, t0=t0, per_trip=per_trip, first=first):
            blocks = [c.gla_blocks + first + (ci - t0) * per_trip + t for t in range(per_trip)]
            z = gate_logits(ci + 1)
            gla_state(ci)
            bcum = gate_cumsum(z)
            for j in blocks:
                proj_half(cur, j, 0)
            gate_scores(ci + 1, bcum)
            for j in blocks:
                proj_half(cur, j, 1)
            norm_rows(xn_ref, nxt, chunk_rows(ci))
            return carry
        lax.fori_loop(t0, t1, gla_body, 0)
    gla_state(n_chunks - 1)
    norm_rows(xn_ref, nxt, chunk_rows(n_chunks - 1))

    rb = CONV_ROWS
    blk_b = c.gla_blocks

    def sc_fill(r, carry):
        rows = pl.ds(pl.multiple_of(r * rb, rb), rb)
        dst = pl.ds(pl.multiple_of(r * rb + SUBLANES, SUBLANES), rb)
        for lt in range(c.sc // LANES):
            b1, l1 = pcol(blk_b, c.sc + lt * LANES, LANES)
            b2, l2 = pcol(blk_b, 2 * c.sc + lt * LANES, LANES)
            ubuf_ref[dst, lt * LANES:(lt + 1) * LANES] = p_ref[b1, rows, l1] * p_ref[b2, rows, l2]
        return carry

    lax.fori_loop(0, tt // rb, sc_fill, 0)

    def sc_body(r, carry):
        base = pl.multiple_of(r * rb, rb)
        rows = pl.ds(base, rb)
        for lt in range(c.sc // LANES):
            ls = slice(lt * LANES, (lt + 1) * LANES)
            win = ubuf_ref[pl.ds(base, rb + SUBLANES), ls]
            conv = _shifted_taps(win, SUBLANES, rb, lambda s: scw_ref[s:s + 1, ls], c.sc_taps)
            b0, l0 = pcol(blk_b, lt * LANES, LANES)
            b3, l3 = pcol(blk_b, 3 * c.sc + lt * LANES, LANES)
            mix_ref[rows, c.vw + lt * LANES:c.vw + (lt + 1) * LANES] = (
                p_ref[b0, rows, l0] * conv * _silu(p_ref[b3, rows, l3])).astype(bf16)
        return carry

    lax.fori_loop(0, tt // rb, sc_body, 0)
    ubuf_ref[0:SUBLANES, :] = ubuf_ref[tt:tt + SUBLANES, :]

    blk_c = c.gla_blocks + c.sc_blocks
    n_lt = c.cf // LANES

    def cf_fill(r, carry):
        rows = pl.ds(pl.multiple_of(r * rb, rb), rb)
        dst = pl.ds(pl.multiple_of(r * rb + CF_HALO, SUBLANES), rb)
        for lt in range(n_lt):
            b0, l0 = pcol(blk_c, lt * LANES, LANES)
            b1, l1 = pcol(blk_c, c.cf + lt * LANES, LANES)
            cbuf_ref[dst, lt * LANES:(lt + 1) * LANES] = p_ref[b0, rows, l0] * _sigmoid(p_ref[b1, rows, l1])
        return carry

    lax.fori_loop(0, tt // rb, cf_fill, 0)

    def cf_rows(r):
        base = pl.multiple_of(r * rb, rb)
        rows = pl.ds(base, rb)
        ys = []
        for lt in range(n_lt):
            ls = slice(lt * LANES, (lt + 1) * LANES)
            win = cbuf_ref[pl.ds(base, rb + CF_HALO), ls]
            y = _shifted_taps(win, CF_HALO, rb, lambda s: cfw_ref[s:s + 1, ls], c.cf_taps)
            ys.append(y + cfb_ref[:, ls])
        tot = ys[0]
        for y in ys[1:]:
            tot = tot + y
        mu = jnp.sum(tot, axis=-1, keepdims=True) * (1.0 / c.cf)
        yc = [y - mu for y in ys]
        sq = yc[0] * yc[0]
        for y in yc[1:]:
            sq = sq + y * y
        var = jnp.sum(sq, axis=-1, keepdims=True) * (1.0 / c.cf)
        inv = lax.rsqrt(var + LN_EPS)
        for lt in range(n_lt):
            ls = slice(lt * LANES, (lt + 1) * LANES)
            yn = yc[lt] * inv * lnw_ref[:, ls] + lnb_ref[:, ls]
            b2, l2 = pcol(blk_c, 2 * c.cf + lt * LANES, LANES)
            mix_ref[rows, c.vw + c.sc + lt * LANES:c.vw + c.sc + (lt + 1) * LANES] = (
                _silu(yn) * _silu(p_ref[b2, rows, l2])).astype(bf16)

    for t0, t1, per_trip, first in _spread(c.gla_blocks, tt // rb):
        def cf_body(r, carry, t0=t0, per_trip=per_trip, first=first):
            cf_rows(r)
            for t in range(per_trip):
                proj_block(nxt, first + (r - t0) * per_trip + t)
            return carry
        lax.fori_loop(t0, t1, cf_body, 0)
    cbuf_ref[0:CF_HALO, :] = cbuf_ref[tt:tt + CF_HALO, :]

    y = jnp.dot(mix_ref[...], wout_ref[...], preferred_element_type=f32)
    out = x_ref[0] + y
    if c.is_last:
        ms = jnp.mean(out * out, axis=-1, keepdims=True)
        out = out * lax.rsqrt(ms + NORM_EPS) * fnw_ref[...]
    o_ref[0] = out


def _pick_time_tile(t_len):
    for tt in (512, 256, 128):
        if t_len % tt == 0:
            return tt
    raise ValueError(f"sequence length {t_len} must be a multiple of 128")


def _const_spec(shape, layer):
    nd = len(shape)
    return pl.BlockSpec((None,) + tuple(shape), lambda b, i: (layer,) + (0,) * nd,
                        pipeline_mode=pl.Buffered(1))


def kernel(x, norm_w, w_in, gla_w_gate_up, gla_b_gate, gla_norm_w, sc_conv_w, cf_conv_w, cf_conv_b,
           cf_ln_w, cf_ln_b, w_out, final_norm_w):
    bsz, t_len, d_model = x.shape
    depth = w_in.shape[0]
    rank, kw = gla_w_gate_up.shape[1:]
    dv = gla_norm_w.shape[-1]
    sc_taps, sc = sc_conv_w.shape[1:]
    cf_taps, cf = cf_conv_w.shape[1:]
    vw = w_out.shape[1] - sc - cf
    heads = vw // dv
    dk = kw // heads
    assert w_in.shape[-1] == 2 * kw + 2 * vw + rank + 4 * sc + 3 * cf
    assert dk == LANES and dv % LANES == 0 and dv <= MXU_COLS and MXU_COLS % dv == 0
    assert sc % MXU_COLS == 0 and cf % MXU_COLS == 0 and kw % MXU_COLS == 0 and vw % MXU_COLS == 0
    assert rank <= LANES and sc_taps <= SUBLANES and cf_taps <= CF_HALO
    tt = _pick_time_tile(t_len)
    n_tiles = t_len // tt

    f32, bf16 = jnp.float32, jnp.bfloat16
    n_gla = 2 * kw + 2 * vw
    w_lr = jnp.pad(w_in[:, :, n_gla:n_gla + rank], ((0, 0), (0, 0), (0, MXU_COLS - rank)))
    w_in_r = jnp.concatenate([w_in[:, :, :n_gla], w_lr, w_in[:, :, n_gla + rank:]], axis=-1).astype(bf16)
    n_blocks = w_in_r.shape[-1] // MXU_COLS
    w_in_r = w_in_r.reshape(depth, d_model, n_blocks, MXU_COLS).transpose(0, 2, 1, 3)
    w_up = jnp.pad(gla_w_gate_up, ((0, 0), (0, LANES - rank), (0, 0))).astype(bf16)
    w_out_b = w_out.astype(bf16)
    sc_w = jnp.pad(sc_conv_w[:, ::-1, :], ((0, 0), (0, SUBLANES - sc_taps), (0, 0)))
    cf_w = jnp.pad(cf_conv_w[:, ::-1, :], ((0, 0), (0, CF_HALO - cf_taps), (0, 0)))
    row = lambda a: a.reshape(a.shape[0], 1, a.shape[-1])
    fnw = final_norm_w.reshape(1, 1, d_model)

    def next_tile(b, i):
        flat = jnp.minimum(b * n_tiles + i + 1, bsz * n_tiles - 1)
        return flat // n_tiles, flat % n_tiles, 0

    for layer in range(depth):
        cfg = _Cfg(d_model=d_model, heads=heads, dk=dk, dv=dv, kw=kw, vw=vw, sc=sc,
                   sc_taps=sc_taps, cf=cf, cf_taps=cf_taps, tt=tt, is_last=(layer == depth - 1))
        assert cfg.n_blocks == n_blocks
        x_spec = pl.BlockSpec((1, tt, d_model), lambda b, i: (b, i, 0))
        x = pl.pallas_call(
            functools.partial(_layer_kernel, cfg=cfg),
            grid=(bsz, n_tiles),
            in_specs=[
                x_spec,
                pl.BlockSpec((1, tt, d_model), next_tile),
                _const_spec((1, d_model), layer),
                _const_spec((n_blocks, d_model, MXU_COLS), layer),
                _const_spec((LANES, kw), layer),
                _const_spec((1, kw), layer),
                _const_spec((1, dv), layer),
                _const_spec((SUBLANES, sc), layer),
                _const_spec((CF_HALO, cf), layer),
                _const_spec((1, cf), layer),
                _const_spec((1, cf), layer),
                _const_spec((1, cf), layer),
                _const_spec((cfg.d_mix, d_model), layer),
                _const_spec((1, d_model), 0),
            ],
            out_specs=x_spec,
            out_shape=jax.ShapeDtypeStruct(x.shape, f32),
            scratch_shapes=[
                pltpu.VMEM((2, tt, d_model), bf16),
                pltpu.VMEM((n_blocks, tt, MXU_COLS), f32),
                pltpu.VMEM((tt, cfg.d_mix), bf16),
                pltpu.VMEM((heads, dk, dv), f32),
                pltpu.VMEM((tt, heads * LANES), bf16),
                pltpu.VMEM((tt, kw), bf16),
                pltpu.VMEM((tt, kw), bf16),
                pltpu.VMEM((tt // GLA_CHUNK * SUBLANES, kw), f32),
                pltpu.VMEM((tt + SUBLANES, sc), f32),
                pltpu.VMEM((tt + CF_HALO, cf), f32),
            ],
            compiler_params=pltpu.CompilerParams(
                dimension_semantics=("arbitrary", "arbitrary"),
                vmem_limit_bytes=VMEM_LIMIT_BYTES),
            name=f"hybrid_layer_{layer}",
        )(x, x, row(norm_w), w_in_r, w_up, row(gla_b_gate), row(gla_norm_w), sc_w, cf_w,
          row(cf_conv_b), row(cf_ln_w), row(cf_ln_b), w_out_b, fnw)
    return x
```
